```python
import jax, jax.numpy as jnp
from jax import lax
import numpy as np

D_MODEL = 1024
BATCH = 8
SEQ = 2048
DEPTH = 4

N_MIXERS = 3
N_A_LAYERS = (DEPTH + 2) // 3
N_B_LAYERS = (DEPTH + 1) // 3
N_C_LAYERS = DEPTH // 3

A_CHUNK = 128
A_GROUPS = 8
A_WIDTH = 3 * D_MODEL
A_GROUP_DIM = A_WIDTH // A_GROUPS
B_WIDTH = D_MODEL
B_WINDOWS = (2, 4, 8, 16)
B_GROUPS = len(B_WINDOWS)
B_GROUP_DIM = B_WIDTH // B_GROUPS
C_WIDTH = D_MODEL
C_KERNEL = 31
MLP_HIDDEN = 4 * D_MODEL
EPS = 1e-6

kernel_name = "interleaved_gmlp_pool_conformer_trunk"


def rmsnorm(x, g):
    xf = x.astype(jnp.float32)
    y = xf * lax.rsqrt(jnp.mean(xf * xf, axis=-1, keepdims=True) + EPS)
    return (y * g.astype(jnp.float32)).astype(x.dtype)


def layernorm(x, g, b):
    xf = x.astype(jnp.float32)
    mu = jnp.mean(xf, axis=-1, keepdims=True)
    var = jnp.mean(jnp.square(xf - mu), axis=-1, keepdims=True)
    y = (xf - mu) * lax.rsqrt(var + EPS)
    return (y * g.astype(jnp.float32) + b.astype(jnp.float32)).astype(x.dtype)


def mixer_a(h, w_in, b_in, vn_g, vn_b, w_s, b_s, w_out):
    bsz, s, _ = h.shape
    z = jax.nn.gelu(h @ w_in + b_in)
    u, v = jnp.split(z, 2, axis=-1)
    v = layernorm(v, vn_g, vn_b)
    v = v.reshape(bsz, s // A_CHUNK, A_CHUNK, A_GROUPS, A_GROUP_DIM)
    causal = jnp.tril(jnp.ones((A_CHUNK, A_CHUNK), dtype=bool))
    w = jnp.where(causal[None], w_s, jnp.zeros_like(w_s))
    gate = jnp.einsum('gts,bcsgd->bctgd', w, v) + b_s.T[None, None, :, :, None]
    gate = gate.reshape(bsz, s, A_WIDTH)
    return (u * gate) @ w_out


def mixer_b(h, w_in, w_grp, b_grp, scale, w_out):
    bsz, s, _ = h.shape
    p = (h @ w_in).astype(jnp.float32)
    cs0 = jnp.pad(jnp.cumsum(p, axis=1), ((0, 0), (1, 0), (0, 0)))
    pos = jnp.arange(s)
    groups = []
    for g, win in enumerate(B_WINDOWS):
        c = cs0[:, :, g * B_GROUP_DIM:(g + 1) * B_GROUP_DIM]
        lagged = jnp.pad(c, ((0, 0), (win - 1, 0), (0, 0)))[:, :s]
        win_sum = c[:, 1:] - lagged
        count = jnp.minimum(pos + 1, win).astype(jnp.float32)
        mean = win_sum / count[None, :, None]
        groups.append(mean - p[:, :, g * B_GROUP_DIM:(g + 1) * B_GROUP_DIM])
    pooled = jnp.stack(groups, axis=2).astype(h.dtype)
    y = jnp.einsum('bsgd,gde->bsge', pooled, w_grp) + b_grp
    y = y.reshape(bsz, s, B_WIDTH) * scale
    return y @ w_out


def mixer_c(h, w_in, w_dw, b_dw, ln_g, ln_b, w_out):
    a, g = jnp.split(h @ w_in, 2, axis=-1)
    y = a * jax.nn.sigmoid(g)
    y = lax.conv_general_dilated(
        y, w_dw[:, None, :].astype(y.dtype), window_strides=(1,),
        padding=[(C_KERNEL - 1, 0)],
        dimension_numbers=('NWC', 'WIO', 'NWC'),
        feature_group_count=C_WIDTH) + b_dw
    y = jax.nn.silu(layernorm(y, ln_g, ln_b))
    return y @ w_out


def channel_mlp(h, w1, w2):
    return jnp.square(jax.nn.relu(h @ w1)) @ w2


def setup_inputs(seed: int = 0) -> dict:
    key = jax.random.key(seed)
    ks = iter(jax.random.split(key, 32))

    def nrm(shape, scale):
        return jax.random.normal(next(ks), shape, jnp.float32) * scale

    def gain(shape, noise=0.05):
        return 1.0 + nrm(shape, noise)

    na, nb, nc = N_A_LAYERS, N_B_LAYERS, N_C_LAYERS
    return {
        "x": nrm((BATCH, SEQ, D_MODEL), 1.0),
        "norm_mix": gain((DEPTH, D_MODEL)),
        "norm_mlp": gain((DEPTH, D_MODEL)),
        "norm_final": gain((D_MODEL,)),
        "a_w_in": nrm((na, D_MODEL, 2 * A_WIDTH), D_MODEL ** -0.5),
        "a_b_in": nrm((na, 2 * A_WIDTH), 0.01),
        "a_vn_g": gain((na, A_WIDTH)),
        "a_vn_b": nrm((na, A_WIDTH), 0.01),
        "a_w_s": nrm((na, A_GROUPS, A_CHUNK, A_CHUNK), A_CHUNK ** -0.5),
        "a_b_s": gain((na, A_GROUPS, A_CHUNK), 0.1),
        "a_w_out": nrm((na, A_WIDTH, D_MODEL), A_WIDTH ** -0.5),
        "b_w_in": nrm((nb, D_MODEL, B_WIDTH), D_MODEL ** -0.5),
        "b_w_grp": nrm((nb, B_GROUPS, B_GROUP_DIM, B_GROUP_DIM), B_GROUP_DIM ** -0.5),
        "b_b_grp": nrm((nb, B_GROUPS, B_GROUP_DIM), 0.01),
        "b_scale": gain((nb, B_WIDTH), 0.1),
        "b_w_out": nrm((nb, B_WIDTH, D_MODEL), B_WIDTH ** -0.5),
        "c_w_in": nrm((nc, D_MODEL, 2 * C_WIDTH), D_MODEL ** -0.5),
        "c_w_dw": nrm((nc, C_KERNEL, C_WIDTH), C_KERNEL ** -0.5),
        "c_b_dw": nrm((nc, C_WIDTH), 0.01),
        "c_ln_g": gain((nc, C_WIDTH)),
        "c_ln_b": nrm((nc, C_WIDTH), 0.01),
        "c_w_out": nrm((nc, C_WIDTH, D_MODEL), C_WIDTH ** -0.5),
        "m_w1": nrm((DEPTH, D_MODEL, MLP_HIDDEN), D_MODEL ** -0.5),
        "m_w2": nrm((DEPTH, MLP_HIDDEN, D_MODEL), MLP_HIDDEN ** -0.5),
    }


def reference(x, norm_mix, norm_mlp, norm_final,
              a_w_in, a_b_in, a_vn_g, a_vn_b, a_w_s, a_b_s, a_w_out,
              b_w_in, b_w_grp, b_b_grp, b_scale, b_w_out,
              c_w_in, c_w_dw, c_b_dw, c_ln_g, c_ln_b, c_w_out,
              m_w1, m_w2):
    ia = ib = ic = 0
    for i in range(DEPTH):
        h = rmsnorm(x, norm_mix[i])
        kind = i % N_MIXERS
        if kind == 0:
            y = mixer_a(h, a_w_in[ia], a_b_in[ia], a_vn_g[ia], a_vn_b[ia],
                        a_w_s[ia], a_b_s[ia], a_w_out[ia])
            ia += 1
        elif kind == 1:
            y = mixer_b(h, b_w_in[ib], b_w_grp[ib], b_b_grp[ib], b_scale[ib], b_w_out[ib])
            ib += 1
        else:
            y = mixer_c(h, c_w_in[ic], c_w_dw[ic], c_b_dw[ic], c_ln_g[ic], c_ln_b[ic], c_w_out[ic])
            ic += 1
        x = x + y
        x = x + channel_mlp(rmsnorm(x, norm_mlp[i]), m_w1[i], m_w2[i])
    return rmsnorm(x, norm_final)
```

```python
import functools
import math

import jax
import jax.numpy as jnp
from jax import lax
from jax.experimental import pallas as pl
from jax.experimental.pallas import tpu as pltpu

EPS = 1e-6
N_MIXERS = 3
A_CHUNK = 128
A_GROUPS = 8
B_WINDOWS = (2, 4, 8, 16)
C_KERNEL = 31

V7X_VMEM_BYTES = 64 * 1024 * 1024
V7X_MXU_DIM = 256
SUBLANES = 8
VMEM_LIMIT_BYTES = 56 * 1024 * 1024

ROW_TILE = 512
B_HALO = 16
C_HALO = 32
CONV_ROWS = 32
CONV_COLS = 256

_GELU_C = math.sqrt(2.0 / math.pi)


def _rms(x, g):
    return x * lax.rsqrt(jnp.mean(x * x, axis=-1, keepdims=True) + EPS) * g


def _gelu(x):
    return 0.5 * x * (1.0 + jnp.tanh(_GELU_C * (x + 0.044715 * (x * x * x))))


def _dot(a, b):
    return jnp.dot(a, b, preferred_element_type=jnp.float32)


def _resident(shape):
    nd = len(shape)
    return pl.BlockSpec(shape, lambda *_: (0,) * nd, pipeline_mode=pl.Buffered(1))


def _mlp_kernel(x_ref, g_ref, w1_ref, w2_ref, gf_ref, o_ref, *, h_chunk, final_norm):
    x = x_ref[...]
    xn = _rms(x, g_ref[...]).astype(jnp.bfloat16)
    acc = x
    hidden = w1_ref.shape[1]
    for c in range(hidden // h_chunk):
        cols = slice(c * h_chunk, (c + 1) * h_chunk)
        h = _dot(xn, w1_ref[:, cols])
        h = jnp.square(jnp.maximum(h, 0.0)).astype(jnp.bfloat16)
        acc = acc + _dot(h, w2_ref[cols, :])
    if final_norm:
        acc = _rms(acc, gf_ref[...])
    o_ref[...] = acc


def _mlp(x, g, w1, w2, gf, final_norm):
    n, d = x.shape
    hidden = w1.shape[1]
    kern = functools.partial(_mlp_kernel, h_chunk=1024, final_norm=final_norm)
    return pl.pallas_call(
        kern,
        out_shape=jax.ShapeDtypeStruct((n, d), jnp.float32),
        grid=(n // ROW_TILE,),
        in_specs=[
            pl.BlockSpec((ROW_TILE, d), lambda i: (i, 0)),
            _resident((1, d)),
            _resident((d, hidden)),
            _resident((hidden, d)),
            _resident((1, d)),
        ],
        out_specs=pl.BlockSpec((ROW_TILE, d), lambda i: (i, 0)),
        compiler_params=pltpu.CompilerParams(
            dimension_semantics=("arbitrary",), vmem_limit_bytes=VMEM_LIMIT_BYTES),
        name="channel_mlp_final" if final_norm else "channel_mlp",
    )(x, g, w1, w2, gf)


def _mixer_a_kernel(x_ref, g_ref, win_ref, bin_ref, vng_ref, vnb_ref, ws_ref, bs_ref,
                    wout_ref, o_ref, v_scr, p_scr, *, col_chunk):
    tm = x_ref.shape[0]
    width = wout_ref.shape[0]
    gdim = width // A_GROUPS
    groups_per_chunk = col_chunk // gdim

    x = x_ref[...]
    xn = _rms(x, g_ref[...]).astype(jnp.bfloat16)

    s1 = jnp.zeros((tm, 1), jnp.float32)
    s2 = jnp.zeros((tm, 1), jnp.float32)
    for c in range(width // col_chunk):
        cols = slice(c * col_chunk, (c + 1) * col_chunk)
        wcols = slice(width + c * col_chunk, width + (c + 1) * col_chunk)
        zv = _gelu(_dot(xn, win_ref[:, wcols]) + bin_ref[:, wcols])
        v_scr[:, cols] = zv
        s1 = s1 + jnp.sum(zv, axis=-1, keepdims=True)
        s2 = s2 + jnp.sum(zv * zv, axis=-1, keepdims=True)
    mu = s1 * (1.0 / width)
    var = s2 * (1.0 / width) - mu * mu
    rstd = lax.rsqrt(var + EPS)

    row = lax.broadcasted_iota(jnp.int32, (A_CHUNK, A_CHUNK), 0)
    col = lax.broadcasted_iota(jnp.int32, (A_CHUNK, A_CHUNK), 1)
    causal = row >= col

    acc = x
    for c in range(width // col_chunk):
        cols = slice(c * col_chunk, (c + 1) * col_chunk)
        u = _gelu(_dot(xn, win_ref[:, cols]) + bin_ref[:, cols])
        vn = ((v_scr[:, cols] - mu) * rstd * vng_ref[:, cols] + vnb_ref[:, cols])
        vn = vn.astype(jnp.bfloat16)
        for gi in range(groups_per_chunk):
            g = c * groups_per_chunk + gi
            lanes = slice(gi * gdim, (gi + 1) * gdim)
            wg = jnp.where(causal, ws_ref[g], 0.0).astype(jnp.bfloat16)
            bg = bs_ref[g]
            for r in range(tm // A_CHUNK):
                rows = slice(r * A_CHUNK, (r + 1) * A_CHUNK)
                gate = _dot(wg, vn[rows, lanes]) + bg
                p_scr[rows, lanes] = (u[rows, lanes] * gate).astype(jnp.bfloat16)
        acc = acc + _dot(p_scr[...], wout_ref[cols, :])
    o_ref[...] = acc


def _mixer_a(x, g, w_in, b_in, vn_g, vn_b, w_s, b_s, w_out):
    n, d = x.shape
    width = w_out.shape[0]
    col_chunk = 3 * V7X_MXU_DIM
    kern = functools.partial(_mixer_a_kernel, col_chunk=col_chunk)
    return pl.pallas_call(
        kern,
        out_shape=jax.ShapeDtypeStruct((n, d), jnp.float32),
        grid=(n // ROW_TILE,),
        in_specs=[
            pl.BlockSpec((ROW_TILE, d), lambda i: (i, 0)),
            _resident((1, d)),
            _resident((d, 2 * width)),
            _resident((1, 2 * width)),
            _resident((1, width)),
            _resident((1, width)),
            _resident((A_GROUPS, A_CHUNK, A_CHUNK)),
            _resident((A_GROUPS, A_CHUNK, 1)),
            _resident((width, d)),
        ],
        out_specs=pl.BlockSpec((ROW_TILE, d), lambda i: (i, 0)),
        scratch_shapes=[
            pltpu.VMEM((ROW_TILE, width), jnp.float32),
            pltpu.VMEM((ROW_TILE, col_chunk), jnp.bfloat16),
        ],
        compiler_params=pltpu.CompilerParams(
            dimension_semantics=("arbitrary",), vmem_limit_bytes=VMEM_LIMIT_BYTES),
        name="mixer_a",
    )(x, g, w_in, b_in, vn_g, vn_b, w_s, b_s, w_out)


def _mixer_b_kernel(x_ref, g_ref, win_ref, wgrp_ref, bgrp_ref, scale_ref, wout_ref,
                    o_ref, ext_scr):
    tm = x_ref.shape[0]
    t = pl.program_id(1)
    gdim = wgrp_ref.shape[1]

    @pl.when(t == 0)
    def _():
        ext_scr[0:B_HALO, :] = jnp.zeros((B_HALO, ext_scr.shape[1]), jnp.float32)

    x = x_ref[...]
    xn = _rms(x, g_ref[...]).astype(jnp.bfloat16)
    ext_scr[B_HALO:B_HALO + tm, :] = _dot(xn, win_ref[...])

    pos = t * tm + lax.broadcasted_iota(jnp.int32, (tm, 1), 0)
    acc = x
    for g, win in enumerate(B_WINDOWS):
        cols = slice(g * gdim, (g + 1) * gdim)
        p = ext_scr[B_HALO:B_HALO + tm, cols]
        s = p
        for j in range(1, win):
            s = s + ext_scr[B_HALO - j:B_HALO - j + tm, cols]
        inv_count = 1.0 / jnp.minimum(pos + 1, win).astype(jnp.float32)
        pooled = (s * inv_count - p).astype(jnp.bfloat16)
        y = _dot(pooled, wgrp_ref[g]) + bgrp_ref[:, cols]
        y = (y * scale_ref[:, cols]).astype(jnp.bfloat16)
        acc = acc + _dot(y, wout_ref[cols, :])
    o_ref[...] = acc
    ext_scr[0:B_HALO, :] = ext_scr[tm:tm + B_HALO, :]


def _mixer_b(x3, g, w_in, w_grp, b_grp, scale, w_out):
    bsz, s, d = x3.shape
    width = w_in.shape[1]
    gdim = w_grp.shape[1]
    return pl.pallas_call(
        _mixer_b_kernel,
        out_shape=jax.ShapeDtypeStruct((bsz, s, d), jnp.float32),
        grid=(bsz, s // ROW_TILE),
        in_specs=[
            pl.BlockSpec((None, ROW_TILE, d), lambda b, t: (b, t, 0)),
            _resident((1, d)),
            _resident((d, width)),
            _resident((len(B_WINDOWS), gdim, gdim)),
            _resident((1, width)),
            _resident((1, width)),
            _resident((width, d)),
        ],
        out_specs=pl.BlockSpec((None, ROW_TILE, d), lambda b, t: (b, t, 0)),
        scratch_shapes=[pltpu.VMEM((B_HALO + ROW_TILE, width), jnp.float32)],
        compiler_params=pltpu.CompilerParams(
            dimension_semantics=("arbitrary", "arbitrary"),
            vmem_limit_bytes=VMEM_LIMIT_BYTES),
        name="mixer_b",
    )(x3, g, w_in, w_grp, b_grp, scale, w_out)


def _mixer_c_kernel(x_ref, g_ref, win_ref, wdw_ref, bdw_ref, lng_ref, lnb_ref, wout_ref,
                    o_ref, ext_scr, conv_scr):
    tm = x_ref.shape[0]
    width = wout_ref.shape[0]
    t = pl.program_id(1)

    @pl.when(t == 0)
    def _():
        ext_scr[0:C_HALO, :] = jnp.zeros((C_HALO, width), jnp.float32)

    x = x_ref[...]
    xn = _rms(x, g_ref[...]).astype(jnp.bfloat16)
    a = _dot(xn, win_ref[:, :width])
    gate = _dot(xn, win_ref[:, width:])
    ext_scr[C_HALO:C_HALO + tm, :] = a * jax.nn.sigmoid(gate)

    lead = C_HALO - (C_KERNEL - 1)

    def conv_block(rb, carry):
        r0 = pl.multiple_of(rb * CONV_ROWS, CONV_ROWS)
        for cb in range(width // CONV_COLS):
            cols = slice(cb * CONV_COLS, (cb + 1) * CONV_COLS)
            acc = jnp.broadcast_to(bdw_ref[:, cols], (CONV_ROWS, CONV_COLS))
            for b in range(SUBLANES):
                rows = CONV_ROWS if b == 0 else CONV_ROWS + SUBLANES
                part = None
                for k in range(C_KERNEL):
                    if (lead + k) % SUBLANES != b:
                        continue
                    start = pl.multiple_of(r0 + (lead + k - b), SUBLANES)
                    term = wdw_ref[k:k + 1, cols] * ext_scr[pl.ds(start, rows), cols]
                    part = term if part is None else part + term
                acc = acc + part[b:b + CONV_ROWS]
            conv_scr[pl.ds(r0, CONV_ROWS), cols] = acc
        return carry

    lax.fori_loop(0, tm // CONV_ROWS, conv_block, 0)

    y = conv_scr[...]
    mu = jnp.mean(y, axis=-1, keepdims=True)
    yc = y - mu
    var = jnp.mean(yc * yc, axis=-1, keepdims=True)
    yn = yc * lax.rsqrt(var + EPS) * lng_ref[...] + lnb_ref[...]
    act = (yn * jax.nn.sigmoid(yn)).astype(jnp.bfloat16)
    o_ref[...] = x + _dot(act, wout_ref[...])
    ext_scr[0:C_HALO, :] = ext_scr[tm:tm + C_HALO, :]


def _mixer_c(x3, g, w_in, w_dw, b_dw, ln_g, ln_b, w_out):
    bsz, s, d = x3.shape
    width = w_out.shape[0]
    return pl.pallas_call(
        _mixer_c_kernel,
        out_shape=jax.ShapeDtypeStruct((bsz, s, d), jnp.float32),
        grid=(bsz, s // ROW_TILE),
        in_specs=[
            pl.BlockSpec((None, ROW_TILE, d), lambda b, t: (b, t, 0)),
            _resident((1, d)),
            _resident((d, 2 * width)),
            _resident((C_KERNEL, width)),
            _resident((1, width)),
            _resident((1, width)),
            _resident((1, width)),
            _resident((width, d)),
        ],
        out_specs=pl.BlockSpec((None, ROW_TILE, d), lambda b, t: (b, t, 0)),
        scratch_shapes=[
            pltpu.VMEM((C_HALO + ROW_TILE, width), jnp.float32),
            pltpu.VMEM((ROW_TILE, width), jnp.float32),
        ],
        compiler_params=pltpu.CompilerParams(
            dimension_semantics=("arbitrary", "arbitrary"),
            vmem_limit_bytes=VMEM_LIMIT_BYTES),
        name="mixer_c",
    )(x3, g, w_in, w_dw, b_dw, ln_g, ln_b, w_out)


def kernel(x, norm_mix, norm_mlp, norm_final, a_w_in, a_b_in, a_vn_g, a_vn_b, a_w_s, a_b_s, a_w_out, b_w_in, b_w_grp, b_b_grp, b_scale, b_w_out, c_w_in, c_w_dw, c_b_dw, c_ln_g, c_ln_b, c_w_out, m_w1, m_w2):
    bsz, s, d = x.shape
    depth = norm_mix.shape[0]
    n = bsz * s
    bf = jnp.bfloat16
    row = lambda v: v.reshape(1, -1)

    h = x.reshape(n, d)
    ia = ib = ic = 0
    for i in range(depth):
        g_mix = row(norm_mix[i])
        kind = i % N_MIXERS
        if kind == 0:
            h = _mixer_a(h, g_mix, a_w_in[ia].astype(bf), row(a_b_in[ia]), row(a_vn_g[ia]),
                         row(a_vn_b[ia]), a_w_s[ia], a_b_s[ia][:, :, None],
                         a_w_out[ia].astype(bf))
            ia += 1
        elif kind == 1:
            h = _mixer_b(h.reshape(bsz, s, d), g_mix, b_w_in[ib].astype(bf),
                         b_w_grp[ib].astype(bf), row(b_b_grp[ib]), row(b_scale[ib]),
                         b_w_out[ib].astype(bf)).reshape(n, d)
            ib += 1
        else:
            h = _mixer_c(h.reshape(bsz, s, d), g_mix, c_w_in[ic].astype(bf), c_w_dw[ic],
                         row(c_b_dw[ic]), row(c_ln_g[ic]), row(c_ln_b[ic]),
                         c_w_out[ic].astype(bf)).reshape(n, d)
            ic += 1
        h = _mlp(h, row(norm_mlp[i]), m_w1[i].astype(bf), m_w2[i].astype(bf),
                 row(norm_final), final_norm=(i == depth - 1))
    return h.reshape(bsz, s, d)
```

```python
import functools
import math

import jax
import jax.numpy as jnp
from jax import lax
from jax.experimental import pallas as pl
from jax.experimental.pallas import tpu as pltpu

EPS = 1e-6
N_MIXERS = 3
A_CHUNK = 128
A_GROUPS = 8
B_WINDOWS = (2, 4, 8, 16)
C_KERNEL = 31

V7X_VMEM_BYTES = 64 * 1024 * 1024
V7X_MXU_DIM = 256
SUBLANES = 8
VMEM_LIMIT_BYTES = 56 * 1024 * 1024

ROW_TILE = 512
MLP_ROW_TILE = 1024
B_HALO = 16
C_HALO = 32
CONV_ROWS = 64
CONV_COLS = 128
C_IN_COLS = 256
C_LN_ROWS = 128

_GELU_C = math.sqrt(2.0 / math.pi)


def _rms(x, g):
    return x * lax.rsqrt(jnp.mean(x * x, axis=-1, keepdims=True) + EPS) * g


def _gelu(x):
    return 0.5 * x * (1.0 + jnp.tanh(_GELU_C * (x + 0.044715 * (x * x * x))))


def _dot(a, b):
    return jnp.dot(a, b, preferred_element_type=jnp.float32)


def _resident(shape, layer=None):
    nd = len(shape)
    if layer is None:
        return pl.BlockSpec(shape, lambda *_: (0,) * nd, pipeline_mode=pl.Buffered(1))
    return pl.BlockSpec((None,) + shape, lambda *_: (layer,) + (0,) * nd,
                        pipeline_mode=pl.Buffered(1))


def _mlp_kernel(x_ref, g_ref, w1_ref, w2_ref, gf_ref, o_ref, *, h_chunk, final_norm):
    x = x_ref[...]
    xn = _rms(x, g_ref[...]).astype(jnp.bfloat16)
    acc = x
    hidden = w1_ref.shape[1]
    for c in range(hidden // h_chunk):
        cols = slice(c * h_chunk, (c + 1) * h_chunk)
        h = _dot(xn, w1_ref[:, cols])
        h = jnp.square(jnp.maximum(h, 0.0)).astype(jnp.bfloat16)
        acc = acc + _dot(h, w2_ref[cols, :])
    if final_norm:
        acc = _rms(acc, gf_ref[...])
    o_ref[...] = acc


def _mlp(x, g, w1, w2, gf, layer, final_norm):
    n, d = x.shape
    hidden = w1.shape[2]
    kern = functools.partial(_mlp_kernel, h_chunk=1024, final_norm=final_norm)
    return pl.pallas_call(
        kern,
        out_shape=jax.ShapeDtypeStruct((n, d), jnp.float32),
        grid=(n // MLP_ROW_TILE,),
        in_specs=[
            pl.BlockSpec((MLP_ROW_TILE, d), lambda i: (i, 0)),
            _resident((1, d)),
            _resident((d, hidden), layer),
            _resident((hidden, d), layer),
            _resident((1, d)),
        ],
        out_specs=pl.BlockSpec((MLP_ROW_TILE, d), lambda i: (i, 0)),
        compiler_params=pltpu.CompilerParams(
            dimension_semantics=("arbitrary",), vmem_limit_bytes=VMEM_LIMIT_BYTES),
        name="channel_mlp_final" if final_norm else "channel_mlp",
    )(x, g, w1, w2, gf)


def _mixer_a_kernel(x_ref, g_ref, win_ref, bin_ref, vng_ref, vnb_ref, ws_ref, bs_ref,
                    wout_ref, o_ref, v_scr, p_scr, *, col_chunk):
    tm = x_ref.shape[0]
    width = wout_ref.shape[0]
    gdim = width // A_GROUPS
    groups_per_chunk = col_chunk // gdim

    x = x_ref[...]
    xn = _rms(x, g_ref[...]).astype(jnp.bfloat16)

    s1 = jnp.zeros((tm, 1), jnp.float32)
    s2 = jnp.zeros((tm, 1), jnp.float32)
    for c in range(width // col_chunk):
        cols = slice(c * col_chunk, (c + 1) * col_chunk)
        wcols = slice(width + c * col_chunk, width + (c + 1) * col_chunk)
        zv = _gelu(_dot(xn, win_ref[:, wcols]) + bin_ref[:, wcols])
        v_scr[:, cols] = zv
        s1 = s1 + jnp.sum(zv, axis=-1, keepdims=True)
        s2 = s2 + jnp.sum(zv * zv, axis=-1, keepdims=True)
    mu = s1 * (1.0 / width)
    var = s2 * (1.0 / width) - mu * mu
    rstd = lax.rsqrt(var + EPS)

    row = lax.broadcasted_iota(jnp.int32, (A_CHUNK, A_CHUNK), 0)
    col = lax.broadcasted_iota(jnp.int32, (A_CHUNK, A_CHUNK), 1)
    causal = row >= col

    acc = x
    for c in range(width // col_chunk):
        cols = slice(c * col_chunk, (c + 1) * col_chunk)
        u = _gelu(_dot(xn, win_ref[:, cols]) + bin_ref[:, cols])
        vn = ((v_scr[:, cols] - mu) * rstd * vng_ref[:, cols] + vnb_ref[:, cols])
        vn = vn.astype(jnp.bfloat16)
        for gi in range(groups_per_chunk):
            g = c * groups_per_chunk + gi
            lanes = slice(gi * gdim, (gi + 1) * gdim)
            wg = jnp.where(causal, ws_ref[g], 0.0).astype(jnp.bfloat16)
            bg = bs_ref[g]
            for r in range(tm // A_CHUNK):
                rows = slice(r * A_CHUNK, (r + 1) * A_CHUNK)
                gate = _dot(wg, vn[rows, lanes]) + bg
                p_scr[rows, lanes] = (u[rows, lanes] * gate).astype(jnp.bfloat16)
        acc = acc + _dot(p_scr[...], wout_ref[cols, :])
    o_ref[...] = acc


def _mixer_a(x, g, w_in, b_in, vn_g, vn_b, w_s, b_s, w_out, layer):
    n, d = x.shape
    width = w_out.shape[1]
    col_chunk = 3 * V7X_MXU_DIM
    kern = functools.partial(_mixer_a_kernel, col_chunk=col_chunk)
    return pl.pallas_call(
        kern,
        out_shape=jax.ShapeDtypeStruct((n, d), jnp.float32),
        grid=(n // ROW_TILE,),
        in_specs=[
            pl.BlockSpec((ROW_TILE, d), lambda i: (i, 0)),
            _resident((1, d)),
            _resident((d, 2 * width), layer),
            _resident((1, 2 * width)),
            _resident((1, width)),
            _resident((1, width)),
            _resident((A_GROUPS, A_CHUNK, A_CHUNK), layer),
            _resident((A_GROUPS, A_CHUNK, 1), layer),
            _resident((width, d), layer),
        ],
        out_specs=pl.BlockSpec((ROW_TILE, d), lambda i: (i, 0)),
        scratch_shapes=[
            pltpu.VMEM((ROW_TILE, width), jnp.float32),
            pltpu.VMEM((ROW_TILE, col_chunk), jnp.bfloat16),
        ],
        compiler_params=pltpu.CompilerParams(
            dimension_semantics=("arbitrary",), vmem_limit_bytes=VMEM_LIMIT_BYTES),
        name="mixer_a",
    )(x, g, w_in, b_in, vn_g, vn_b, w_s, b_s, w_out)


def _mixer_b_kernel(x_ref, g_ref, win_ref, wgrp_ref, bgrp_ref, scale_ref, wout_ref,
                    o_ref, ext_scr):
    tm = x_ref.shape[0]
    t = pl.program_id(1)
    gdim = wgrp_ref.shape[1]

    @pl.when(t == 0)
    def _():
        ext_scr[0:B_HALO, :] = jnp.zeros((B_HALO, ext_scr.shape[1]), jnp.float32)

    x = x_ref[...]
    xn = _rms(x, g_ref[...]).astype(jnp.bfloat16)
    ext_scr[B_HALO:B_HALO + tm, :] = _dot(xn, win_ref[...])

    pos = t * tm + lax.broadcasted_iota(jnp.int32, (tm, 1), 0)
    acc = x
    for g, win in enumerate(B_WINDOWS):
        cols = slice(g * gdim, (g + 1) * gdim)
        p = ext_scr[B_HALO:B_HALO + tm, cols]
        s = p
        for j in range(1, win):
            s = s + ext_scr[B_HALO - j:B_HALO - j + tm, cols]
        inv_count = 1.0 / jnp.minimum(pos + 1, win).astype(jnp.float32)
        pooled = (s * inv_count - p).astype(jnp.bfloat16)
        y = _dot(pooled, wgrp_ref[g]) + bgrp_ref[:, cols]
        y = (y * scale_ref[:, cols]).astype(jnp.bfloat16)
        acc = acc + _dot(y, wout_ref[cols, :])
    o_ref[...] = acc
    ext_scr[0:B_HALO, :] = ext_scr[tm:tm + B_HALO, :]


def _mixer_b(x3, g, w_in, w_grp, b_grp, scale, w_out, layer):
    bsz, s, d = x3.shape
    width = w_in.shape[2]
    gdim = w_grp.shape[2]
    return pl.pallas_call(
        _mixer_b_kernel,
        out_shape=jax.ShapeDtypeStruct((bsz, s, d), jnp.float32),
        grid=(bsz, s // ROW_TILE),
        in_specs=[
            pl.BlockSpec((None, ROW_TILE, d), lambda b, t: (b, t, 0)),
            _resident((1, d)),
            _resident((d, width), layer),
            _resident((len(B_WINDOWS), gdim, gdim), layer),
            _resident((1, width)),
            _resident((1, width)),
            _resident((width, d), layer),
        ],
        out_specs=pl.BlockSpec((None, ROW_TILE, d), lambda b, t: (b, t, 0)),
        scratch_shapes=[pltpu.VMEM((B_HALO + ROW_TILE, width), jnp.float32)],
        compiler_params=pltpu.CompilerParams(
            dimension_semantics=("arbitrary", "arbitrary"),
            vmem_limit_bytes=VMEM_LIMIT_BYTES),
        name="mixer_b",
    )(x3, g, w_in, w_grp, b_grp, scale, w_out)


def _mixer_c_kernel(xc_ref, xp_ref, g_ref, win_ref, wdw_ref, bdw_ref, lng_ref, lnb_ref,
                    wout_ref, o_ref, ext_scr, conv_scr, act_scr, *, tiles_per_seq):
    tm = xc_ref.shape[0]
    width = wout_ref.shape[0]
    i = pl.program_id(0)
    cur = i % 2
    prev = 1 - cur

    @pl.when(i == 0)
    def _():
        ext_scr[1] = jnp.zeros(ext_scr.shape[1:], jnp.float32)

    xn = _rms(xc_ref[...], g_ref[...]).astype(jnp.bfloat16)
    for c in range(width // C_IN_COLS):
        cols = slice(c * C_IN_COLS, (c + 1) * C_IN_COLS)
        gcols = slice(width + c * C_IN_COLS, width + (c + 1) * C_IN_COLS)
        a = _dot(xn, win_ref[:, cols])
        gate = _dot(xn, win_ref[:, gcols])
        ext_scr[cur, C_HALO:C_HALO + tm, cols] = a * jax.nn.sigmoid(gate)
    seq_start = (i % tiles_per_seq) == 0
    tail = ext_scr[prev, tm:tm + C_HALO, :]
    ext_scr[cur, 0:C_HALO, :] = jnp.where(seq_start, 0.0, tail)

    lead = C_HALO - (C_KERNEL - 1)
    for rc in range(tm // C_LN_ROWS):
        for rb in range(C_LN_ROWS // CONV_ROWS):
            r0 = rc * C_LN_ROWS + rb * CONV_ROWS
            for cb in range(width // CONV_COLS):
                cols = slice(cb * CONV_COLS, (cb + 1) * CONV_COLS)
                acc = jnp.broadcast_to(bdw_ref[:, cols], (CONV_ROWS, CONV_COLS))
                for b in range(SUBLANES):
                    rows = CONV_ROWS if b == 0 else CONV_ROWS + SUBLANES
                    part = None
                    for k in range(C_KERNEL):
                        if (lead + k) % SUBLANES != b:
                            continue
                        start = r0 + (lead + k - b)
                        term = wdw_ref[k:k + 1, cols] * ext_scr[prev, start:start + rows, cols]
                        part = term if part is None else part + term
                    acc = acc + part[b:b + CONV_ROWS]
                conv_scr[r0:r0 + CONV_ROWS, cols] = acc
        rows = slice(rc * C_LN_ROWS, (rc + 1) * C_LN_ROWS)
        y = conv_scr[rows, :]
        mu = jnp.mean(y, axis=-1, keepdims=True)
        yc = y - mu
        var = jnp.mean(yc * yc, axis=-1, keepdims=True)
        yn = yc * lax.rsqrt(var + EPS) * lng_ref[...] + lnb_ref[...]
        act_scr[rows, :] = (yn * jax.nn.sigmoid(yn)).astype(jnp.bfloat16)
    o_ref[...] = xp_ref[...] + _dot(act_scr[...], wout_ref[...])


def _mixer_c(x, g, w_in, w_dw, b_dw, ln_g, ln_b, w_out, layer, seq_len):
    n, d = x.shape
    width = w_out.shape[1]
    n_tiles = n // ROW_TILE
    kern = functools.partial(_mixer_c_kernel, tiles_per_seq=seq_len // ROW_TILE)
    return pl.pallas_call(
        kern,
        out_shape=jax.ShapeDtypeStruct((n, d), jnp.float32),
        grid=(n_tiles + 1,),
        in_specs=[
            pl.BlockSpec((ROW_TILE, d), lambda i: (jnp.minimum(i, n_tiles - 1), 0)),
            pl.BlockSpec((ROW_TILE, d), lambda i: (jnp.maximum(i - 1, 0), 0)),
            _resident((1, d)),
            _resident((d, 2 * width), layer),
            _resident((C_KERNEL, width), layer),
            _resident((1, width)),
            _resident((1, width)),
            _resident((1, width)),
            _resident((width, d), layer),
        ],
        out_specs=pl.BlockSpec((ROW_TILE, d), lambda i: (jnp.maximum(i - 1, 0), 0)),
        scratch_shapes=[
            pltpu.VMEM((2, C_HALO + ROW_TILE, width), jnp.float32),
            pltpu.VMEM((ROW_TILE, width), jnp.float32),
            pltpu.VMEM((ROW_TILE, width), jnp.bfloat16),
        ],
        compiler_params=pltpu.CompilerParams(
            dimension_semantics=("arbitrary",), vmem_limit_bytes=VMEM_LIMIT_BYTES),
        name="mixer_c",
    )(x, x, g, w_in, w_dw, b_dw, ln_g, ln_b, w_out)


def kernel(x, norm_mix, norm_mlp, norm_final, a_w_in, a_b_in, a_vn_g, a_vn_b, a_w_s, a_b_s, a_w_out, b_w_in, b_w_grp, b_b_grp, b_scale, b_w_out, c_w_in, c_w_dw, c_b_dw, c_ln_g, c_ln_b, c_w_out, m_w1, m_w2):
    bsz, s, d = x.shape
    depth = norm_mix.shape[0]
    n = bsz * s
    bf = jnp.bfloat16
    row = lambda v: v.reshape(1, -1)

    a_w_in, a_w_out = a_w_in.astype(bf), a_w_out.astype(bf)
    b_w_in, b_w_grp, b_w_out = b_w_in.astype(bf), b_w_grp.astype(bf), b_w_out.astype(bf)
    c_w_in, c_w_out = c_w_in.astype(bf), c_w_out.astype(bf)
    m_w1, m_w2 = m_w1.astype(bf), m_w2.astype(bf)
    a_b_s = a_b_s[:, :, :, None]

    h = x.reshape(n, d)
    ia = ib = ic = 0
    for i in range(depth):
        g_mix = row(norm_mix[i])
        kind = i % N_MIXERS
        if kind == 0:
            h = _mixer_a(h, g_mix, a_w_in, row(a_b_in[ia]), row(a_vn_g[ia]),
                         row(a_vn_b[ia]), a_w_s, a_b_s, a_w_out, ia)
            ia += 1
        elif kind == 1:
            h = _mixer_b(h.reshape(bsz, s, d), g_mix, b_w_in, b_w_grp, row(b_b_grp[ib]),
                         row(b_scale[ib]), b_w_out, ib).reshape(n, d)
            ib += 1
        else:
            h = _mixer_c(h, g_mix, c_w_in, c_w_dw, row(c_b_dw[ic]), row(c_ln_g[ic]),
                         row(c_ln_b[ic]), c_w_out, ic, s)
            ic += 1
        h = _mlp(h, row(norm_mlp[i]), m_w1, m_w2, row(norm_final), i,
                 final_norm=(i == depth - 1))
    return h.reshape(bsz, s, d)
```

```python
import functools
import math

import jax
import jax.numpy as jnp
from jax import lax
from jax.experimental import pallas as pl
from jax.experimental.pallas import tpu as pltpu

EPS = 1e-6
N_MIXERS = 3
A_CHUNK = 128
A_GROUPS = 8
B_WINDOWS = (2, 4, 8, 16)
C_KERNEL = 31

V7X_MXU_DIM = 256
SUBLANES = 8
BF16_SUBLANES = 16
VMEM_LIMIT_BYTES = 56 * 1024 * 1024

ROW_TILE = 512
MLP_ROW_TILE = 1024
MLP_H_CHUNK = 1024
A_V_CHUNK = 256
B_HALO = 32
C_HALO = 32
CONV_ROWS = 64
CONV_COLS = 128
C_IN_COLS = 256
C_LN_ROWS = 128

_GELU_C = math.sqrt(2.0 / math.pi)


def _rms(x, g):
    return x * lax.rsqrt(jnp.mean(x * x, axis=-1, keepdims=True) + EPS) * g


def _gelu(x):
    t = jnp.tanh(x * (_GELU_C + (_GELU_C * 0.044715) * (x * x)))
    hx = 0.5 * x
    return hx + hx * t


def _dot(a, b):
    return jnp.dot(a, b, preferred_element_type=jnp.float32)


def _resident(shape, layer=None):
    nd = len(shape)
    if layer is None:
        return pl.BlockSpec(shape, lambda *_: (0,) * nd, pipeline_mode=pl.Buffered(1))
    return pl.BlockSpec((None,) + shape, lambda *_: (layer,) + (0,) * nd,
                        pipeline_mode=pl.Buffered(1))


def _cast_specs(stacks, layer, n_blocks, step_of):
    in_specs, out_specs, out_shapes = [], [], []
    for w in stacks:
        _, rows, cols = w.shape
        assert rows % (n_blocks * BF16_SUBLANES) == 0, (rows, n_blocks)
        rb = rows // n_blocks
        in_specs.append(pl.BlockSpec((None, rb, cols), lambda *ids: (layer, step_of(*ids), 0)))
        out_specs.append(pl.BlockSpec((rb, cols), lambda *ids: (step_of(*ids), 0)))
        out_shapes.append(jax.ShapeDtypeStruct((rows, cols), jnp.bfloat16))
    return in_specs, out_specs, out_shapes


def _split_refs(rest, n_cast):
    cast_in = rest[:n_cast]
    o_ref = rest[n_cast]
    cast_out = rest[n_cast + 1:2 * n_cast + 1]
    scratch = rest[2 * n_cast + 1:]
    return cast_in, o_ref, cast_out, scratch


def _cast_blocks(cast_in, cast_out):
    for src, dst in zip(cast_in, cast_out):
        dst[...] = src[...].astype(dst.dtype)


def _mlp_kernel(x_ref, g_ref, w1_ref, w2_ref, gf_ref, *rest, n_cast, final_norm):
    cast_in, o_ref, cast_out, _ = _split_refs(rest, n_cast)
    x = x_ref[...]
    xn = _rms(x, g_ref[...]).astype(jnp.bfloat16)
    acc = x
    hidden = w1_ref.shape[1]
    for c in range(hidden // MLP_H_CHUNK):
        cols = slice(c * MLP_H_CHUNK, (c + 1) * MLP_H_CHUNK)
        h = _dot(xn, w1_ref[:, cols])
        h = jnp.square(jnp.maximum(h, 0.0)).astype(jnp.bfloat16)
        acc = acc + _dot(h, w2_ref[cols, :])
    if final_norm:
        acc = _rms(acc, gf_ref[...])
    o_ref[...] = acc
    _cast_blocks(cast_in, cast_out)


def _mlp(x, g, w1, w2, gf, final_norm, next_stacks=(), next_layer=0):
    n, d = x.shape
    hidden = w1.shape[1]
    n_steps = n // MLP_ROW_TILE
    c_in, c_out, c_shapes = _cast_specs(next_stacks, next_layer, n_steps, lambda i: i)
    kern = functools.partial(_mlp_kernel, n_cast=len(next_stacks), final_norm=final_norm)
    return pl.pallas_call(
        kern,
        out_shape=[jax.ShapeDtypeStruct((n, d), jnp.float32)] + c_shapes,
        grid=(n_steps,),
        in_specs=[
            pl.BlockSpec((MLP_ROW_TILE, d), lambda i: (i, 0)),
            _resident((1, d)),
            _resident((d, hidden)),
            _resident((hidden, d)),
            _resident((1, d)),
        ] + c_in,
        out_specs=[pl.BlockSpec((MLP_ROW_TILE, d), lambda i: (i, 0))] + c_out,
        compiler_params=pltpu.CompilerParams(
            dimension_semantics=("arbitrary",), vmem_limit_bytes=VMEM_LIMIT_BYTES),
        name="channel_mlp_final" if final_norm else "channel_mlp",
    )(x, g, w1, w2, gf, *next_stacks)


def _mixer_a_kernel(x_ref, g_ref, win_ref, bin_ref, vng_ref, vnb_ref, ws_ref, bs_ref,
                    wout_ref, *rest, n_cast, col_chunk):
    cast_in, o_ref, cast_out, (v_scr, p_scr) = _split_refs(rest, n_cast)
    tm = x_ref.shape[0]
    width = wout_ref.shape[0]
    gdim = width // A_GROUPS
    groups_per_chunk = col_chunk // gdim

    x = x_ref[...]
    xn = _rms(x, g_ref[...]).astype(jnp.bfloat16)

    s1 = jnp.zeros((tm, 1), jnp.float32)
    s2 = jnp.zeros((tm, 1), jnp.float32)
    for c in range(width // A_V_CHUNK):
        cols = slice(c * A_V_CHUNK, (c + 1) * A_V_CHUNK)
        wcols = slice(width + c * A_V_CHUNK, width + (c + 1) * A_V_CHUNK)
        zv = _gelu(_dot(xn, win_ref[:, wcols]) + bin_ref[:, wcols])
        v_scr[:, cols] = zv
        s1 = s1 + jnp.sum(zv, axis=-1, keepdims=True)
        s2 = s2 + jnp.sum(zv * zv, axis=-1, keepdims=True)
    mu = s1 * (1.0 / width)
    var = s2 * (1.0 / width) - mu * mu
    rstd = lax.rsqrt(var + EPS)

    row = lax.broadcasted_iota(jnp.int32, (A_CHUNK, A_CHUNK), 0)
    col = lax.broadcasted_iota(jnp.int32, (A_CHUNK, A_CHUNK), 1)
    causal = row >= col

    acc = x
    for c in range(width // col_chunk):
        cols = slice(c * col_chunk, (c + 1) * col_chunk)
        u = _gelu(_dot(xn, win_ref[:, cols]) + bin_ref[:, cols])
        vn = ((v_scr[:, cols] - mu) * rstd * vng_ref[:, cols] + vnb_ref[:, cols])
        vn = vn.astype(jnp.bfloat16)
        for gi in range(groups_per_chunk):
            g = c * groups_per_chunk + gi
            lanes = slice(gi * gdim, (gi + 1) * gdim)
            wg = jnp.where(causal, ws_ref[g], 0.0).astype(jnp.bfloat16)
            bg = bs_ref[g]
            for r in range(tm // A_CHUNK):
                rows = slice(r * A_CHUNK, (r + 1) * A_CHUNK)
                gate = _dot(wg, vn[rows, lanes]) + bg
                p_scr[rows, lanes] = (u[rows, lanes] * gate).astype(jnp.bfloat16)
        acc = acc + _dot(p_scr[...], wout_ref[cols, :])
    o_ref[...] = acc
    _cast_blocks(cast_in, cast_out)


def _mixer_a(x, g, w_in, b_in, vn_g, vn_b, w_s, b_s, w_out, layer, next_stacks, next_layer):
    n, d = x.shape
    width = w_out.shape[0]
    col_chunk = 3 * V7X_MXU_DIM
    n_steps = n // ROW_TILE
    c_in, c_out, c_shapes = _cast_specs(next_stacks, next_layer, n_steps, lambda i: i)
    kern = functools.partial(_mixer_a_kernel, n_cast=len(next_stacks), col_chunk=col_chunk)
    return pl.pallas_call(
        kern,
        out_shape=[jax.ShapeDtypeStruct((n, d), jnp.float32)] + c_shapes,
        grid=(n_steps,),
        in_specs=[
            pl.BlockSpec((ROW_TILE, d), lambda i: (i, 0)),
            _resident((1, d)),
            _resident((d, 2 * width)),
            _resident((1, 2 * width)),
            _resident((1, width)),
            _resident((1, width)),
            _resident((A_GROUPS, A_CHUNK, A_CHUNK), layer),
            _resident((A_GROUPS, A_CHUNK, 1), layer),
            _resident((width, d)),
        ] + c_in,
        out_specs=[pl.BlockSpec((ROW_TILE, d), lambda i: (i, 0))] + c_out,
        scratch_shapes=[
            pltpu.VMEM((ROW_TILE, width), jnp.float32),
            pltpu.VMEM((ROW_TILE, col_chunk), jnp.bfloat16),
        ],
        compiler_params=pltpu.CompilerParams(
            dimension_semantics=("arbitrary",), vmem_limit_bytes=VMEM_LIMIT_BYTES),
        name="mixer_a",
    )(x, g, w_in, b_in, vn_g, vn_b, w_s, b_s, w_out, *next_stacks)


def _mixer_b_kernel(x_ref, g_ref, win_ref, wgrp_ref, bgrp_ref, scale_ref, wout_ref,
                    *rest, n_cast):
    cast_in, o_ref, cast_out, (ext_scr,) = _split_refs(rest, n_cast)
    tm = x_ref.shape[0]
    t = pl.program_id(1)
    gdim = wgrp_ref.shape[1]

    @pl.when(t == 0)
    def _():
        ext_scr[0:B_HALO, :] = jnp.zeros((B_HALO, ext_scr.shape[1]), jnp.float32)

    x = x_ref[...]
    xn = _rms(x, g_ref[...]).astype(jnp.bfloat16)
    ext_scr[B_HALO:B_HALO + tm, :] = _dot(xn, win_ref[...])

    pos = t * tm + lax.broadcasted_iota(jnp.int32, (tm, 1), 0)
    acc = x
    for g, win in enumerate(B_WINDOWS):
        cols = slice(g * gdim, (g + 1) * gdim)
        p = ext_scr[B_HALO:B_HALO + tm, cols]
        levels = win.bit_length() - 1
        s = ext_scr[B_HALO - SUBLANES * levels:B_HALO + tm, cols]
        for lvl in range(levels):
            lag = 1 << lvl
            n_rows = s.shape[0] - SUBLANES
            s = s[SUBLANES:] + s[SUBLANES - lag:SUBLANES - lag + n_rows]
        inv_count = 1.0 / jnp.minimum(pos + 1, win).astype(jnp.float32)
        pooled = (s * inv_count - p).astype(jnp.bfloat16)
        y = _dot(pooled, wgrp_ref[g]) + bgrp_ref[:, cols]
        y = (y * scale_ref[:, cols]).astype(jnp.bfloat16)
        acc = acc + _dot(y, wout_ref[cols, :])
    o_ref[...] = acc
    ext_scr[0:B_HALO, :] = ext_scr[tm:tm + B_HALO, :]
    _cast_blocks(cast_in, cast_out)


def _mixer_b(x3, g, w_in, w_grp, b_grp, scale, w_out, next_stacks, next_layer):
    bsz, s, d = x3.shape
    width = w_in.shape[1]
    gdim = w_grp.shape[1]
    tiles_per_seq = s // ROW_TILE
    c_in, c_out, c_shapes = _cast_specs(next_stacks, next_layer, bsz * tiles_per_seq,
                                        lambda b, t: b * tiles_per_seq + t)
    kern = functools.partial(_mixer_b_kernel, n_cast=len(next_stacks))
    return pl.pallas_call(
        kern,
        out_shape=[jax.ShapeDtypeStruct((bsz, s, d), jnp.float32)] + c_shapes,
        grid=(bsz, tiles_per_seq),
        in_specs=[
            pl.BlockSpec((None, ROW_TILE, d), lambda b, t: (b, t, 0)),
            _resident((1, d)),
            _resident((d, width)),
            _resident((len(B_WINDOWS), gdim, gdim)),
            _resident((1, width)),
            _resident((1, width)),
            _resident((width, d)),
        ] + c_in,
        out_specs=[pl.BlockSpec((None, ROW_TILE, d), lambda b, t: (b, t, 0))] + c_out,
        scratch_shapes=[pltpu.VMEM((B_HALO + ROW_TILE, width), jnp.float32)],
        compiler_params=pltpu.CompilerParams(
            dimension_semantics=("arbitrary", "arbitrary"),
            vmem_limit_bytes=VMEM_LIMIT_BYTES),
        name="mixer_b",
    )(x3, g, w_in, w_grp, b_grp, scale, w_out, *next_stacks)


def _mixer_c_kernel(xc_ref, xp_ref, g_ref, win_ref, wdw_ref, bdw_ref, lng_ref, lnb_ref,
                    wout_ref, *rest, n_cast, tiles_per_seq):
    cast_in, o_ref, cast_out, (ext_scr, conv_scr, act_scr) = _split_refs(rest, n_cast)
    tm = xc_ref.shape[0]
    width = wout_ref.shape[0]
    i = pl.program_id(0)
    cur = i % 2
    prev = 1 - cur

    @pl.when(i == 0)
    def _():
        ext_scr[1] = jnp.zeros(ext_scr.shape[1:], jnp.float32)

    xn = _rms(xc_ref[...], g_ref[...]).astype(jnp.bfloat16)
    for c in range(width // C_IN_COLS):
        cols = slice(c * C_IN_COLS, (c + 1) * C_IN_COLS)
        gcols = slice(width + c * C_IN_COLS, width + (c + 1) * C_IN_COLS)
        a = _dot(xn, win_ref[:, cols])
        gate = _dot(xn, win_ref[:, gcols])
        ext_scr[cur, C_HALO:C_HALO + tm, cols] = a * jax.nn.sigmoid(gate)
    seq_start = (i % tiles_per_seq) == 0
    tail = ext_scr[prev, tm:tm + C_HALO, :]
    ext_scr[cur, 0:C_HALO, :] = jnp.where(seq_start, 0.0, tail)

    lead = C_HALO - (C_KERNEL - 1)
    for rc in range(tm // C_LN_ROWS):
        for rb in range(C_LN_ROWS // CONV_ROWS):
            r0 = rc * C_LN_ROWS + rb * CONV_ROWS
            for cb in range(width // CONV_COLS):
                cols = slice(cb * CONV_COLS, (cb + 1) * CONV_COLS)
                acc = jnp.broadcast_to(bdw_ref[:, cols], (CONV_ROWS, CONV_COLS))
                for b in range(SUBLANES):
                    rows = CONV_ROWS if b == 0 else CONV_ROWS + SUBLANES
                    part = None
                    for k in range(C_KERNEL):
                        if (lead + k) % SUBLANES != b:
                            continue
                        start = r0 + (lead + k - b)
                        term = wdw_ref[k:k + 1, cols] * ext_scr[prev, start:start + rows, cols]
                        part = term if part is None else part + term
                    acc = acc + part[b:b + CONV_ROWS]
                conv_scr[r0:r0 + CONV_ROWS, cols] = acc
        rows = slice(rc * C_LN_ROWS, (rc + 1) * C_LN_ROWS)
        y = conv_scr[rows, :]
        mu = jnp.mean(y, axis=-1, keepdims=True)
        yc = y - mu
        var = jnp.mean(yc * yc, axis=-1, keepdims=True)
        yn = yc * lax.rsqrt(var + EPS) * lng_ref[...] + lnb_ref[...]
        act_scr[rows, :] = (yn * jax.nn.sigmoid(yn)).astype(jnp.bfloat16)
    o_ref[...] = xp_ref[...] + _dot(act_scr[...], wout_ref[...])
    _cast_blocks(cast_in, cast_out)


def _mixer_c(x, g, w_in, w_dw, b_dw, ln_g, ln_b, w_out, layer, seq_len, next_stacks,
             next_layer):
    n, d = x.shape
    width = w_out.shape[0]
    n_tiles = n // ROW_TILE
    last = n_tiles - 1
    c_in, c_out, c_shapes = _cast_specs(next_stacks, next_layer, n_tiles,
                                        lambda i: jnp.minimum(i, last))
    kern = functools.partial(_mixer_c_kernel, n_cast=len(next_stacks),
                             tiles_per_seq=seq_len // ROW_TILE)
    return pl.pallas_call(
        kern,
        out_shape=[jax.ShapeDtypeStruct((n, d), jnp.float32)] + c_shapes,
        grid=(n_tiles + 1,),
        in_specs=[
            pl.BlockSpec((ROW_TILE, d), lambda i: (jnp.minimum(i, last), 0)),
            pl.BlockSpec((ROW_TILE, d), lambda i: (jnp.maximum(i - 1, 0), 0)),
            _resident((1, d)),
            _resident((d, 2 * width)),
            _resident((C_KERNEL, width), layer),
            _resident((1, width)),
            _resident((1, width)),
            _resident((1, width)),
            _resident((width, d)),
        ] + c_in,
        out_specs=[pl.BlockSpec((ROW_TILE, d), lambda i: (jnp.maximum(i - 1, 0), 0))] + c_out,
        scratch_shapes=[
            pltpu.VMEM((2, C_HALO + ROW_TILE, width), jnp.float32),
            pltpu.VMEM((ROW_TILE, width), jnp.float32),
            pltpu.VMEM((ROW_TILE, width), jnp.bfloat16),
        ],
        compiler_params=pltpu.CompilerParams(
            dimension_semantics=("arbitrary",), vmem_limit_bytes=VMEM_LIMIT_BYTES),
        name="mixer_c",
    )(x, x, g, w_in, w_dw, b_dw, ln_g, ln_b, w_out, *next_stacks)


def kernel(x, norm_mix, norm_mlp, norm_final, a_w_in, a_b_in, a_vn_g, a_vn_b, a_w_s, a_b_s, a_w_out, b_w_in, b_w_grp, b_b_grp, b_scale, b_w_out, c_w_in, c_w_dw, c_b_dw, c_ln_g, c_ln_b, c_w_out, m_w1, m_w2):
    bsz, s, d = x.shape
    depth = norm_mix.shape[0]
    n = bsz * s
    row = lambda v: v.reshape(1, -1)

    a_b_s = a_b_s[:, :, :, None]
    n_grp, gdim = b_w_grp.shape[1], b_w_grp.shape[2]
    b_w_grp2 = b_w_grp.reshape(b_w_grp.shape[0], n_grp * gdim, gdim)

    def mixer_stacks(i):
        kind, idx = i % N_MIXERS, i // N_MIXERS
        stacks = ((a_w_in, a_w_out), (b_w_in, b_w_grp2, b_w_out), (c_w_in, c_w_out))[kind]
        return stacks, idx

    stacks, idx = mixer_stacks(0)
    mixer_w = [w[idx].astype(jnp.bfloat16) for w in stacks]

    h = x.reshape(n, d)
    for i in range(depth):
        g_mix = row(norm_mix[i])
        kind, idx = i % N_MIXERS, i // N_MIXERS
        mlp_stacks = (m_w1, m_w2)
        if kind == 0:
            h, w1, w2 = _mixer_a(h, g_mix, mixer_w[0], row(a_b_in[idx]), row(a_vn_g[idx]),
                                 row(a_vn_b[idx]), a_w_s, a_b_s, mixer_w[1], idx,
                                 mlp_stacks, i)
        elif kind == 1:
            h, w1, w2 = _mixer_b(h.reshape(bsz, s, d), g_mix, mixer_w[0],
                                 mixer_w[1].reshape(n_grp, gdim, gdim), row(b_b_grp[idx]),
                                 row(b_scale[idx]), mixer_w[2], mlp_stacks, i)
            h = h.reshape(n, d)
        else:
            h, w1, w2 = _mixer_c(h, g_mix, mixer_w[0], c_w_dw, row(c_b_dw[idx]),
                                 row(c_ln_g[idx]), row(c_ln_b[idx]), mixer_w[1], idx, s,
                                 mlp_stacks, i)
        last = i == depth - 1
        next_stacks, next_idx = ((), 0) if last else mixer_stacks(i + 1)
        h, *mixer_w = _mlp(h, row(norm_mlp[i]), w1, w2, row(norm_final), last,
                           next_stacks, next_idx)
    return h.reshape(bsz, s, d)
```
